```python
import math
import jax, jax.numpy as jnp
from jax import lax
import numpy as np

D_MODEL = 2048
BATCH = 1
SEQ = 8192
DEPTH = 1

ATTN_WIDTH = D_MODEL // 2
CONV_WIDTH = D_MODEL - ATTN_WIDTH
D_MIX = ATTN_WIDTH + CONV_WIDTH
DIFF_HEAD_DIM = 64
N_DIFF_HEADS = ATTN_WIDTH // (2 * DIFF_HEAD_DIM)
CONV_KERNEL = 31
Q_BLOCK = 128
PEER_HEADS = 8
PEER_KEYS = 128
N_EXPERTS = PEER_KEYS * PEER_KEYS
PEER_TOPK = 16
PEER_KEY_DIM = 256
PEER_TOK_BLOCK = 64
EPS = 1e-6
IN_COLS = 3 * ATTN_WIDTH + 2 * CONV_WIDTH

kernel_name = "hybrid_diffattn_conformer_peer_block"


def rms_norm(x, g):
    xf = x.astype(jnp.float32)
    y = xf * lax.rsqrt(jnp.mean(xf * xf, axis=-1, keepdims=True) + EPS)
    return (y * g.astype(jnp.float32)).astype(x.dtype)


def layer_norm(x, g, b):
    xf = x.astype(jnp.float32)
    mu = jnp.mean(xf, axis=-1, keepdims=True)
    var = jnp.mean(jnp.square(xf - mu), axis=-1, keepdims=True)
    y = (xf - mu) * lax.rsqrt(var + EPS)
    return (y * g.astype(jnp.float32) + b.astype(jnp.float32)).astype(x.dtype)


def lambda_init_fn(layer_idx):
    return 0.8 - 0.6 * math.exp(-0.3 * layer_idx)


def alibi_slopes(n_heads):
    return 2.0 ** (-8.0 * (np.arange(n_heads, dtype=np.float32) + 1.0) / n_heads)


def diff_attention(qkv, lq1, lk1, lq2, lk2, g_subln, lambda_init):
    B, S, _ = qkv.shape
    H, Dh = N_DIFF_HEADS, DIFF_HEAD_DIM
    q, k, v = jnp.split(qkv, 3, axis=-1)
    q = q.reshape(B, S, H, 2, Dh).transpose(0, 2, 3, 1, 4)
    k = k.reshape(B, S, H, 2, Dh).transpose(0, 2, 3, 1, 4)
    v = v.reshape(B, S, H, 2 * Dh).transpose(0, 2, 1, 3)
    lam = (jnp.exp(jnp.sum(lq1.astype(jnp.float32) * lk1.astype(jnp.float32)))
           - jnp.exp(jnp.sum(lq2.astype(jnp.float32) * lk2.astype(jnp.float32)))
           + lambda_init)
    slopes = jnp.asarray(alibi_slopes(H))
    scale = 1.0 / math.sqrt(Dh)
    n_blocks = S // Q_BLOCK
    q_blocks = jnp.moveaxis(q.reshape(B, H, 2, n_blocks, Q_BLOCK, Dh), 3, 0)
    starts = jnp.arange(n_blocks, dtype=jnp.int32) * Q_BLOCK
    k_pos = jnp.arange(S, dtype=jnp.int32)

    def block_fn(args):
        qb, start = args
        q_pos = start + jnp.arange(Q_BLOCK, dtype=jnp.int32)
        s = jnp.einsum('bhmqd,bhmkd->bhmqk', qb, k).astype(jnp.float32) * scale
        dist = (q_pos[:, None] - k_pos[None, :])
        bias = -slopes[:, None, None, None] * dist.astype(jnp.float32)
        s = jnp.where(dist >= 0, s + bias, -jnp.inf)
        p = jax.nn.softmax(s, axis=-1)
        a = p[:, :, 0] - lam * p[:, :, 1]
        return jnp.einsum('bhqk,bhkd->bhqd', a.astype(v.dtype), v)

    o = lax.map(block_fn, (q_blocks, starts))
    o = o.transpose(1, 0, 3, 2, 4).reshape(B, S, H, 2 * Dh)
    o = rms_norm(o, g_subln) * (1.0 - lambda_init)
    return o.reshape(B, S, ATTN_WIDTH)


def conformer_conv(glu_in, conv_w, conv_b, ln_g, ln_b):
    a, gate = jnp.split(glu_in, 2, axis=-1)
    u = a * jax.nn.sigmoid(gate)
    y = lax.conv_general_dilated(
        u, conv_w[:, None, :].astype(u.dtype), window_strides=(1,),
        padding=[(CONV_KERNEL - 1, 0)],
        dimension_numbers=('NWC', 'WIO', 'NWC'),
        feature_group_count=CONV_WIDTH)
    y = y + conv_b
    y = layer_norm(y, ln_g, ln_b)
    return jax.nn.silu(y)


def peer_ffn(h, w_q, k1, k2, u_tab, v_tab):
    B, S, D = h.shape
    T = B * S
    half = PEER_KEY_DIM // 2
    q = (h @ w_q).reshape(T, PEER_HEADS, PEER_KEY_DIM)
    hf = h.reshape(T, D)
    nb = T // PEER_TOK_BLOCK
    q_b = q.reshape(nb, PEER_TOK_BLOCK, PEER_HEADS, PEER_KEY_DIM)
    h_b = hf.reshape(nb, PEER_TOK_BLOCK, D)

    def block_fn(args):
        hb, qb = args
        s1 = jnp.einsum('thd,kd->thk', qb[..., :half], k1)
        s2 = jnp.einsum('thd,kd->thk', qb[..., half:], k2)
        v1, i1 = lax.top_k(s1, PEER_TOPK)
        v2, i2 = lax.top_k(s2, PEER_TOPK)
        cand = (v1[..., :, None] + v2[..., None, :]).reshape(
            PEER_TOK_BLOCK, PEER_HEADS, PEER_TOPK * PEER_TOPK)
        cand_idx = (i1[..., :, None] * PEER_KEYS + i2[..., None, :]).reshape(
            PEER_TOK_BLOCK, PEER_HEADS, PEER_TOPK * PEER_TOPK)
        top, pos = lax.top_k(cand, PEER_TOPK)
        idx = jnp.take_along_axis(cand_idx, pos, axis=-1)
        g = jax.nn.softmax(top.astype(jnp.float32), axis=-1)
        u = u_tab[idx]
        vv = v_tab[idx]
        act = jax.nn.gelu(jnp.einsum('thkd,td->thk', u, hb).astype(jnp.float32),
                          approximate=False) * g
        return jnp.einsum('thk,thkd->td', act.astype(vv.dtype), vv)

    out = lax.map(block_fn, (h_b, q_b))
    return out.reshape(B, S, D)


def setup_inputs(seed: int = 0) -> dict:
    key = jax.random.key(seed)
    ks = jax.random.split(key, 24)
    D = D_MODEL
    f32 = jnp.float32
    nrm = lambda k, shape, s: jax.random.normal(k, shape, f32) * s
    gain = lambda k, shape: 1.0 + nrm(k, shape, 0.05)
    return {
        "x": nrm(ks[0], (BATCH, SEQ, D), 1.0),
        "c": nrm(ks[1], (BATCH, D), 1.0),
        "w_ada": nrm(ks[2], (DEPTH, D, 6 * D), 0.2 * D ** -0.5),
        "b_ada": nrm(ks[3], (DEPTH, 6 * D), 0.01),
        "g_pre_mix": gain(ks[4], (DEPTH, D)),
        "g_post_mix": gain(ks[5], (DEPTH, D)),
        "g_pre_ffn": gain(ks[6], (DEPTH, D)),
        "g_post_ffn": gain(ks[7], (DEPTH, D)),
        "w_in": nrm(ks[8], (DEPTH, D, IN_COLS), D ** -0.5),
        "lambda_q1": nrm(ks[9], (DEPTH, DIFF_HEAD_DIM), 0.1),
        "lambda_k1": nrm(ks[10], (DEPTH, DIFF_HEAD_DIM), 0.1),
        "lambda_q2": nrm(ks[11], (DEPTH, DIFF_HEAD_DIM), 0.1),
        "lambda_k2": nrm(ks[12], (DEPTH, DIFF_HEAD_DIM), 0.1),
        "g_subln": gain(ks[13], (DEPTH, 2 * DIFF_HEAD_DIM)),
        "conv_w": nrm(ks[14], (DEPTH, CONV_KERNEL, CONV_WIDTH), CONV_KERNEL ** -0.5),
        "conv_b": nrm(ks[15], (DEPTH, CONV_WIDTH), 0.01),
        "conv_ln_g": gain(ks[16], (DEPTH, CONV_WIDTH)),
        "conv_ln_b": nrm(ks[17], (DEPTH, CONV_WIDTH), 0.01),
        "w_out": nrm(ks[18], (DEPTH, D_MIX, D), D_MIX ** -0.5),
        "peer_wq": nrm(ks[19], (DEPTH, D, PEER_HEADS * PEER_KEY_DIM), D ** -0.5),
        "peer_k1": nrm(ks[20], (DEPTH, PEER_KEYS, PEER_KEY_DIM // 2), (PEER_KEY_DIM // 2) ** -0.5),
        "peer_k2": nrm(ks[21], (DEPTH, PEER_KEYS, PEER_KEY_DIM // 2), (PEER_KEY_DIM // 2) ** -0.5),
        "peer_u": nrm(ks[22], (DEPTH, N_EXPERTS, D), D ** -0.5),
        "peer_v": nrm(ks[23], (DEPTH, N_EXPERTS, D), (PEER_HEADS * PEER_TOPK) ** -0.5),
    }


def reference(x, c, w_ada, b_ada, g_pre_mix, g_post_mix, g_pre_ffn, g_post_ffn,
              w_in, lambda_q1, lambda_k1, lambda_q2, lambda_k2, g_subln,
              conv_w, conv_b, conv_ln_g, conv_ln_b, w_out,
              peer_wq, peer_k1, peer_k2, peer_u, peer_v):
    for l in range(DEPTH):
        lambda_init = lambda_init_fn(l)
        mod = jax.nn.silu(c) @ w_ada[l] + b_ada[l]
        sh_m, sc_m, gt_m, sh_f, sc_f, gt_f = [m[:, None, :] for m in jnp.split(mod, 6, axis=-1)]

        h = rms_norm(x, g_pre_mix[l]) * (1.0 + sc_m) + sh_m
        proj = h @ w_in[l]
        qkv = proj[..., :3 * ATTN_WIDTH]
        glu_in = proj[..., 3 * ATTN_WIDTH:]
        y_attn = diff_attention(qkv, lambda_q1[l], lambda_k1[l], lambda_q2[l], lambda_k2[l],
                                g_subln[l], lambda_init)
        y_conv = conformer_conv(glu_in, conv_w[l], conv_b[l], conv_ln_g[l], conv_ln_b[l])
        mix = jnp.concatenate([y_attn, y_conv], axis=-1) @ w_out[l]
        x = x + gt_m * rms_norm(mix, g_post_mix[l])

        h = rms_norm(x, g_pre_ffn[l]) * (1.0 + sc_f) + sh_f
        ff = peer_ffn(h, peer_wq[l], peer_k1[l], peer_k2[l], peer_u[l], peer_v[l])
        x = x + gt_f * rms_norm(ff, g_post_ffn[l])
    return x
```

```python
import functools
import math

import numpy as np
import jax
import jax.numpy as jnp
from jax.experimental import pallas as pl
from jax.experimental.pallas import tpu as pltpu

EPS = 1e-6
DIFF_HEAD_DIM = 64
HEAD_W = 2 * DIFF_HEAD_DIM
CONV_KERNEL = 31
CONV_HALO = 32
PEER_HEADS = 8
PEER_KEYS = 128
PEER_TOPK = 16
PEER_HALF = 128

VMEM_LIMIT_BYTES = 56 * 1024 * 1024

_F32 = jnp.float32
_BF16 = jnp.bfloat16
_NEG_INF = float("-inf")
_POS_INF = float("inf")


def _params(*sem):
    return pltpu.CompilerParams(dimension_semantics=sem, vmem_limit_bytes=VMEM_LIMIT_BYTES)


def _rms(x):
    return x * jax.lax.rsqrt(jnp.mean(x * x, axis=-1, keepdims=True) + EPS)


def _ada_kernel(c_ref, w_ref, b_ref, o_ref):
    c = c_ref[...]
    s = c * jax.nn.sigmoid(c)
    o_ref[...] = jnp.sum(w_ref[...] * s, axis=0, keepdims=True) + b_ref[...]


def _ada(c_col, w, b, bn=1024):
    d, n = w.shape
    return pl.pallas_call(
        _ada_kernel,
        grid=(n // bn,),
        in_specs=[
            pl.BlockSpec((d, 1), lambda j: (0, 0)),
            pl.BlockSpec((d, bn), lambda j: (0, j)),
            pl.BlockSpec((1, bn), lambda j: (0, j)),
        ],
        out_specs=pl.BlockSpec((1, bn), lambda j: (0, j)),
        out_shape=jax.ShapeDtypeStruct((1, n), _F32),
        compiler_params=_params("parallel"),
        name="ada",
    )(c_col, w, b)


def _inproj_kernel(x_ref, g_ref, sc_ref, sh_ref, w_ref, o_ref, h_ref):
    @pl.when(pl.program_id(1) == 0)
    def _():
        h = (_rms(x_ref[...]) * g_ref[...]) * (1.0 + sc_ref[...]) + sh_ref[...]
        h_ref[...] = h.astype(_BF16)

    o_ref[...] = jnp.dot(h_ref[...], w_ref[...], preferred_element_type=_F32).astype(o_ref.dtype)


def _inproj(x, g, sc, sh, w, out_dtype, tm=512, tn=1024):
    s, d = x.shape
    n = w.shape[1]
    vec = pl.BlockSpec((1, d), lambda i, j: (0, 0))
    return pl.pallas_call(
        _inproj_kernel,
        grid=(s // tm, n // tn),
        in_specs=[
            pl.BlockSpec((tm, d), lambda i, j: (i, 0)),
            vec, vec, vec,
            pl.BlockSpec((d, tn), lambda i, j: (0, j)),
        ],
        out_specs=pl.BlockSpec((tm, tn), lambda i, j: (i, j)),
        out_shape=jax.ShapeDtypeStruct((s, n), out_dtype),
        scratch_shapes=[pltpu.VMEM((tm, d), _BF16)],
        compiler_params=_params("parallel", "arbitrary"),
        name="inproj",
    )(x, g, sc, sh, w)


def _attn_kernel(qi_ref, kj_ref, slope_ref,
                 q_ref, k_ref, v_ref, lq1_ref, lk1_ref, lq2_ref, lk2_ref, gs_ref,
                 o_ref,
                 qlo_ref, qhi_ref, boff_ref, bdiag_ref, m_ref, l_ref, acc_ref,
                 *, lambda_init, blk):
    h = pl.program_id(0)
    p = pl.program_id(1)
    i = qi_ref[p]
    j = kj_ref[p]
    slope = slope_ref[h]

    @pl.when(j == 0)
    def _init():
        qs = q_ref[...] * jnp.asarray(1.0 / math.sqrt(DIFF_HEAD_DIM), _BF16)
        lane = jax.lax.broadcasted_iota(jnp.int32, qs.shape, 1)
        zero = jnp.zeros_like(qs)
        qlo_ref[...] = jnp.where(lane < DIFF_HEAD_DIM, qs, zero)
        qhi_ref[...] = jnp.where(lane >= DIFF_HEAD_DIM, qs, zero)
        r = jax.lax.broadcasted_iota(jnp.int32, (blk, blk), 0)
        c = jax.lax.broadcasted_iota(jnp.int32, (blk, blk), 1)
        d = (c - r).astype(_F32) * slope
        boff_ref[...] = d
        bdiag_ref[...] = jnp.where(c > r, _NEG_INF, d)
        m_ref[...] = jnp.full(m_ref.shape, _NEG_INF, _F32)
        l_ref[...] = jnp.zeros(l_ref.shape, _F32)
        acc_ref[...] = jnp.zeros(acc_ref.shape, _F32)

    def _step(bias_ref):
        k = k_ref[...]
        v = v_ref[...]
        off = slope * ((j - i) * blk).astype(_F32)
        for idx, qx_ref in ((0, qlo_ref), (1, qhi_ref)):
            z = jax.lax.dot_general(qx_ref[...], k, (((1,), (1,)), ((), ())),
                                    preferred_element_type=_F32) + bias_ref[...]
            m_prev = m_ref[idx]
            m_new = jnp.maximum(m_prev, jnp.max(z, axis=1, keepdims=True) + off)
            alpha = jnp.exp(m_prev - m_new)
            pexp = jnp.exp(z - (m_new - off))
            l_ref[idx] = alpha * l_ref[idx] + jnp.sum(pexp, axis=1, keepdims=True)
            acc_ref[idx] = alpha * acc_ref[idx] + jnp.dot(
                pexp.astype(_BF16), v, preferred_element_type=_F32)
            m_ref[idx] = m_new

    @pl.when(j < i)
    def _off_diag():
        _step(boff_ref)

    @pl.when(j == i)
    def _diag():
        _step(bdiag_ref)
        lam = (jnp.exp(jnp.sum(lq1_ref[...] * lk1_ref[...], axis=-1, keepdims=True))
               - jnp.exp(jnp.sum(lq2_ref[...] * lk2_ref[...], axis=-1, keepdims=True))
               + lambda_init)
        o = acc_ref[0] / l_ref[0] - lam * (acc_ref[1] / l_ref[1])
        o = (_rms(o) * gs_ref[...]) * (1.0 - lambda_init)
        o_ref[...] = o.astype(o_ref.dtype)


def _attention(qkv, lq1, lk1, lq2, lk2, g_subln, lambda_init, n_heads, blk=512):
    s = qkv.shape[0]
    nb = s // blk
    qi = np.concatenate([np.full(i + 1, i, np.int32) for i in range(nb)])
    kj = np.concatenate([np.arange(i + 1, dtype=np.int32) for i in range(nb)])
    slopes = (2.0 ** (-8.0 * (np.arange(n_heads, dtype=np.float32) + 1.0) / n_heads)).astype(np.float32)
    small = lambda n: pl.BlockSpec((1, n), lambda h, p, qi, kj, sl: (0, 0))
    grid_spec = pltpu.PrefetchScalarGridSpec(
        num_scalar_prefetch=3,
        grid=(n_heads, qi.shape[0]),
        in_specs=[
            pl.BlockSpec((blk, HEAD_W), lambda h, p, qi, kj, sl: (qi[p], h)),
            pl.BlockSpec((blk, HEAD_W), lambda h, p, qi, kj, sl: (kj[p], n_heads + h)),
            pl.BlockSpec((blk, HEAD_W), lambda h, p, qi, kj, sl: (kj[p], 2 * n_heads + h)),
            small(DIFF_HEAD_DIM), small(DIFF_HEAD_DIM), small(DIFF_HEAD_DIM), small(DIFF_HEAD_DIM),
            small(HEAD_W),
        ],
        out_specs=pl.BlockSpec((blk, HEAD_W), lambda h, p, qi, kj, sl: (qi[p], h)),
        scratch_shapes=[
            pltpu.VMEM((blk, HEAD_W), _BF16),
            pltpu.VMEM((blk, HEAD_W), _BF16),
            pltpu.VMEM((blk, blk), _F32),
            pltpu.VMEM((blk, blk), _F32),
            pltpu.VMEM((2, blk, 1), _F32),
            pltpu.VMEM((2, blk, 1), _F32),
            pltpu.VMEM((2, blk, HEAD_W), _F32),
        ],
    )
    return pl.pallas_call(
        functools.partial(_attn_kernel, lambda_init=lambda_init, blk=blk),
        grid_spec=grid_spec,
        out_shape=jax.ShapeDtypeStruct((s, n_heads * HEAD_W), _BF16),
        compiler_params=_params("parallel", "arbitrary"),
        name="attn",
    )(jnp.asarray(qi), jnp.asarray(kj), jnp.asarray(slopes),
      qkv, qkv, qkv, lq1, lk1, lq2, lk2, g_subln)


def _conv_kernel(cur_ref, prev_ref, w_ref, b_ref, lg_ref, lb_ref, o_ref, u_ref, *, tm, ch):
    i = pl.program_id(0)
    cur = cur_ref[...]
    u_ref[pl.ds(CONV_HALO, tm), :] = cur[:, :ch] * jax.nn.sigmoid(cur[:, ch:])
    prev = prev_ref[...]
    u_prev = prev[:, :ch] * jax.nn.sigmoid(prev[:, ch:])
    u_ref[pl.ds(0, CONV_HALO), :] = jnp.where(i > 0, u_prev, jnp.zeros_like(u_prev))
    acc = jnp.zeros((tm, ch), _F32)
    for tap in range(CONV_KERNEL):
        acc = acc + w_ref[tap:tap + 1, :] * u_ref[pl.ds(CONV_HALO - (CONV_KERNEL - 1) + tap, tm), :]
    y = acc + b_ref[...]
    mu = jnp.mean(y, axis=-1, keepdims=True)
    yc = y - mu
    var = jnp.mean(yc * yc, axis=-1, keepdims=True)
    yn = (yc * jax.lax.rsqrt(var + EPS)) * lg_ref[...] + lb_ref[...]
    o_ref[...] = (yn * jax.nn.sigmoid(yn)).astype(o_ref.dtype)


def _conv(glu, w, b, lg, lb, tm=256):
    s, two_ch = glu.shape
    ch = two_ch // 2
    ratio = tm // CONV_HALO
    vec = pl.BlockSpec((1, ch), lambda i: (0, 0))
    return pl.pallas_call(
        functools.partial(_conv_kernel, tm=tm, ch=ch),
        grid=(s // tm,),
        in_specs=[
            pl.BlockSpec((tm, two_ch), lambda i: (i, 0)),
            pl.BlockSpec((CONV_HALO, two_ch), lambda i: (jnp.maximum(i * ratio - 1, 0), 0)),
            pl.BlockSpec((CONV_KERNEL, ch), lambda i: (0, 0)),
            vec, vec, vec,
        ],
        out_specs=pl.BlockSpec((tm, ch), lambda i: (i, 0)),
        out_shape=jax.ShapeDtypeStruct((s, ch), _BF16),
        scratch_shapes=[pltpu.VMEM((tm + CONV_HALO, ch), _F32)],
        compiler_params=_params("parallel"),
        name="conv",
    )(glu, glu, w, b, lg, lb)


def _outproj_kernel(ya_ref, yc_ref, wa_ref, wc_ref, x_ref, gpost_ref, gt_ref, gpre_ref, sc_ref, sh_ref,
                    x1_ref, h2_ref):
    mix = (jnp.dot(ya_ref[...], wa_ref[...], preferred_element_type=_F32)
           + jnp.dot(yc_ref[...], wc_ref[...], preferred_element_type=_F32))
    x1 = x_ref[...] + gt_ref[...] * (_rms(mix) * gpost_ref[...])
    x1_ref[...] = x1
    h2 = (_rms(x1) * gpre_ref[...]) * (1.0 + sc_ref[...]) + sh_ref[...]
    h2_ref[...] = h2.astype(h2_ref.dtype)


def _outproj(ya, yc, wa, wc, x, gpost, gt, gpre, sc, sh, tm=512):
    s, d = x.shape
    ka, kc = ya.shape[1], yc.shape[1]
    vec = pl.BlockSpec((1, d), lambda i: (0, 0))
    return pl.pallas_call(
        _outproj_kernel,
        grid=(s // tm,),
        in_specs=[
            pl.BlockSpec((tm, ka), lambda i: (i, 0)),
            pl.BlockSpec((tm, kc), lambda i: (i, 0)),
            pl.BlockSpec((ka, d), lambda i: (0, 0)),
            pl.BlockSpec((kc, d), lambda i: (0, 0)),
            pl.BlockSpec((tm, d), lambda i: (i, 0)),
            vec, vec, vec, vec, vec,
        ],
        out_specs=[pl.BlockSpec((tm, d), lambda i: (i, 0)), pl.BlockSpec((tm, d), lambda i: (i, 0))],
        out_shape=[jax.ShapeDtypeStruct((s, d), _F32), jax.ShapeDtypeStruct((s, d), _BF16)],
        compiler_params=_params("parallel"),
        name="outproj",
    )(ya, yc, wa, wc, x, gpost, gt, gpre, sc, sh)


def _route_kernel(h_ref, wq_ref, k1_ref, k2_ref,
                  s2_ref, e2_ref, th_ref, e1n_ref,
                  v1_ref, v2_ref, cand_ref):
    q = jnp.dot(h_ref[...], wq_ref[...], preferred_element_type=_F32)
    nt = (((1,), (1,)), ((), ()))
    hi = jax.lax.Precision.HIGHEST
    s1 = jax.lax.dot_general(k1_ref[...], q[:, :PEER_HALF], nt, precision=hi,
                             preferred_element_type=_F32)
    s2 = jax.lax.dot_general(k2_ref[...], q[:, PEER_HALF:], nt, precision=hi,
                             preferred_element_type=_F32)

    def top_values(s, out_ref):
        work = s
        for it in range(PEER_TOPK):
            m = jnp.max(work, axis=0, keepdims=True)
            out_ref[it:it + 1, :] = m
            if it + 1 < PEER_TOPK:
                work = jnp.where(work == m, _NEG_INF, work)

    top_values(s1, v1_ref)
    top_values(s2, v2_ref)
    v1 = v1_ref[...]
    v2 = v2_ref[...]
    for a in range(PEER_TOPK):
        cand_ref[a * PEER_TOPK:(a + 1) * PEER_TOPK, :] = v1[a:a + 1, :] + v2
    cand = cand_ref[...]
    work = cand
    tau = None
    for it in range(PEER_TOPK):
        tau = jnp.max(work, axis=0, keepdims=True)
        if it + 1 < PEER_TOPK:
            work = jnp.where(work == tau, _NEG_INF, work)
    m0 = v1[0:1, :] + v2[0:1, :]
    z = jnp.sum(jnp.where(cand >= tau, jnp.exp(cand - m0), 0.0), axis=0, keepdims=True)
    theta = jnp.full(s1.shape, _POS_INF, _F32)
    for a in range(PEER_TOPK):
        ca = cand[a * PEER_TOPK:(a + 1) * PEER_TOPK, :]
        th_a = jnp.min(jnp.where(ca >= tau, v2, _POS_INF), axis=0, keepdims=True)
        theta = jnp.minimum(theta, jnp.where(s1 == v1[a:a + 1, :], th_a, _POS_INF))
    s2_ref[0] = s2
    e2_ref[0] = jnp.exp(s2 - v2[0:1, :])
    th_ref[0] = theta
    e1n_ref[0] = jnp.exp(s1 - v1[0:1, :]) / z


def _route(h2, wq, k1, k2, tm=512):
    s, d = h2.shape
    n_heads = wq.shape[1] // (2 * PEER_HALF)
    out = jax.ShapeDtypeStruct((n_heads, PEER_KEYS, s), _F32)
    ospec = pl.BlockSpec((1, PEER_KEYS, tm), lambda i, h: (h, 0, i))
    kspec = pl.BlockSpec((PEER_KEYS, PEER_HALF), lambda i, h: (0, 0))
    return pl.pallas_call(
        _route_kernel,
        grid=(s // tm, n_heads),
        in_specs=[
            pl.BlockSpec((tm, d), lambda i, h: (i, 0)),
            pl.BlockSpec((d, 2 * PEER_HALF), lambda i, h: (0, h)),
            kspec, kspec,
        ],
        out_specs=[ospec, ospec, ospec, ospec],
        out_shape=[out, out, out, out],
        scratch_shapes=[
            pltpu.VMEM((PEER_TOPK, tm), _F32),
            pltpu.VMEM((PEER_TOPK, tm), _F32),
            pltpu.VMEM((PEER_TOPK * PEER_TOPK, tm), _F32),
        ],
        compiler_params=_params("parallel", "arbitrary"),
        name="route",
    )(h2, wq, k1, k2)


def _peer_kernel(h_ref, u_ref, vt_ref, s2_ref, e2_ref, th_ref, e1n_ref, o_ref, w_ref, *, n_heads, rows):
    @pl.when(pl.program_id(1) == 0)
    def _():
        o_ref[...] = jnp.zeros(o_ref.shape, _F32)

    hu = jax.lax.dot_general(u_ref[...], h_ref[...], (((1,), (1,)), ((), ())),
                             preferred_element_type=_F32)
    act = 0.5 * hu * (1.0 + jax.lax.erf(hu * (1.0 / math.sqrt(2.0))))
    for r in range(rows):
        g = None
        for hd in range(n_heads):
            sel = jnp.where(s2_ref[hd] >= th_ref[hd, r:r + 1, :], e2_ref[hd], 0.0)
            term = sel * e1n_ref[hd, r:r + 1, :]
            g = term if g is None else g + term
        w_ref[r * PEER_KEYS:(r + 1) * PEER_KEYS, :] = (
            act[r * PEER_KEYS:(r + 1) * PEER_KEYS, :] * g).astype(_BF16)
    o_ref[...] += jnp.dot(vt_ref[...], w_ref[...], preferred_element_type=_F32)


def _peer(h2, u, vt, s2t, e2t, tht, e1nt, tb=512, ec=1024):
    s, d = h2.shape
    n_exp = u.shape[0]
    n_heads = s2t.shape[0]
    rows = ec // PEER_KEYS
    full = pl.BlockSpec((n_heads, PEER_KEYS, tb), lambda i, j: (0, 0, i))
    part = pl.BlockSpec((n_heads, rows, tb), lambda i, j: (0, j, i))
    return pl.pallas_call(
        functools.partial(_peer_kernel, n_heads=n_heads, rows=rows),
        grid=(s // tb, n_exp // ec),
        in_specs=[
            pl.BlockSpec((tb, d), lambda i, j: (i, 0)),
            pl.BlockSpec((ec, d), lambda i, j: (j, 0)),
            pl.BlockSpec((d, ec), lambda i, j: (0, j)),
            full, full, part, part,
        ],
        out_specs=pl.BlockSpec((d, tb), lambda i, j: (0, i)),
        out_shape=jax.ShapeDtypeStruct((d, s), _F32),
        scratch_shapes=[pltpu.VMEM((ec, tb), _BF16)],
        compiler_params=_params("parallel", "arbitrary"),
        name="peer",
    )(h2, u, vt, s2t, e2t, tht, e1nt)


def _final_kernel(fft_ref, x1_ref, g_ref, gt_ref, o_ref):
    ff = fft_ref[...].T
    o_ref[...] = x1_ref[...] + gt_ref[...] * (_rms(ff) * g_ref[...])


def _final(fft, x1, g, gt, tm=512):
    s, d = x1.shape
    vec = pl.BlockSpec((1, d), lambda i: (0, 0))
    return pl.pallas_call(
        _final_kernel,
        grid=(s // tm,),
        in_specs=[
            pl.BlockSpec((d, tm), lambda i: (0, i)),
            pl.BlockSpec((tm, d), lambda i: (i, 0)),
            vec, vec,
        ],
        out_specs=pl.BlockSpec((tm, d), lambda i: (i, 0)),
        out_shape=jax.ShapeDtypeStruct((s, d), _F32),
        compiler_params=_params("parallel"),
        name="final",
    )(fft, x1, g, gt)


def kernel(x, c, w_ada, b_ada, g_pre_mix, g_post_mix, g_pre_ffn, g_post_ffn, w_in, lambda_q1, lambda_k1, lambda_q2, lambda_k2, g_subln, conv_w, conv_b, conv_ln_g, conv_ln_b, w_out, peer_wq, peer_k1, peer_k2, peer_u, peer_v):
    batch, seq, d = x.shape
    depth = w_ada.shape[0]
    attn_w = d // 2
    n_heads = attn_w // HEAD_W
    assert c.shape == (batch, d) and w_in.shape[2] == 3 * attn_w + 2 * (d - attn_w)
    row = lambda a: a.reshape(1, -1)

    outs = []
    for bi in range(batch):
        xb = x[bi]
        c_col = c[bi].reshape(d, 1)
        for l in range(depth):
            lambda_init = 0.8 - 0.6 * math.exp(-0.3 * l)
            mod = _ada(c_col, w_ada[l], row(b_ada[l]))
            sh_m, sc_m, gt_m, sh_f, sc_f, gt_f = [mod[:, k * d:(k + 1) * d] for k in range(6)]

            w_in_b = w_in[l].astype(_BF16)
            g_pre = row(g_pre_mix[l])
            qkv = _inproj(xb, g_pre, sc_m, sh_m, w_in_b[:, :3 * attn_w], _BF16)
            glu = _inproj(xb, g_pre, sc_m, sh_m, w_in_b[:, 3 * attn_w:], _F32)
            y_attn = _attention(qkv, row(lambda_q1[l]), row(lambda_k1[l]), row(lambda_q2[l]),
                                row(lambda_k2[l]), row(g_subln[l]), lambda_init, n_heads)
            y_conv = _conv(glu, conv_w[l], row(conv_b[l]), row(conv_ln_g[l]), row(conv_ln_b[l]))
            w_out_b = w_out[l].astype(_BF16)
            x1, h2 = _outproj(y_attn, y_conv, w_out_b[:attn_w], w_out_b[attn_w:], xb,
                              row(g_post_mix[l]), gt_m, row(g_pre_ffn[l]), sc_f, sh_f)

            s2t, e2t, tht, e1nt = _route(h2, peer_wq[l].astype(_BF16), peer_k1[l], peer_k2[l])
            fft = _peer(h2, peer_u[l].astype(_BF16), peer_v[l].T.astype(_BF16), s2t, e2t, tht, e1nt)
            xb = _final(fft, x1, row(g_post_ffn[l]), gt_f)
        outs.append(xb)
    return jnp.stack(outs, axis=0)
```

```python
import functools
import math

import numpy as np
import jax
import jax.numpy as jnp
from jax.experimental import pallas as pl
from jax.experimental.pallas import tpu as pltpu

EPS = 1e-6
DIFF_HEAD_DIM = 64
HEAD_W = 2 * DIFF_HEAD_DIM
CONV_KERNEL = 31
CONV_HALO = 32
PEER_HEADS = 8
PEER_KEYS = 128
PEER_TOPK = 16
PEER_HALF = 128

VMEM_LIMIT_BYTES = 56 * 1024 * 1024

_F32 = jnp.float32
_BF16 = jnp.bfloat16
_NEG_INF = float("-inf")
_POS_INF = float("inf")


def _params(*sem):
    return pltpu.CompilerParams(dimension_semantics=sem, vmem_limit_bytes=VMEM_LIMIT_BYTES)


def _rms(x):
    return x * jax.lax.rsqrt(jnp.mean(x * x, axis=-1, keepdims=True) + EPS)


def _ada_kernel(c_ref, w_ref, b_ref, o_ref):
    c = c_ref[...]
    s = c * jax.nn.sigmoid(c)
    o_ref[...] = jnp.sum(w_ref[...] * s, axis=0, keepdims=True) + b_ref[...]


def _ada(c_col, w, b, bn=1024):
    d, n = w.shape
    return pl.pallas_call(
        _ada_kernel,
        grid=(n // bn,),
        in_specs=[
            pl.BlockSpec((d, 1), lambda j: (0, 0)),
            pl.BlockSpec((d, bn), lambda j: (0, j)),
            pl.BlockSpec((1, bn), lambda j: (0, j)),
        ],
        out_specs=pl.BlockSpec((1, bn), lambda j: (0, j)),
        out_shape=jax.ShapeDtypeStruct((1, n), _F32),
        compiler_params=_params("parallel"),
        name="ada",
    )(c_col, w, b)


def _inproj_kernel(x_ref, g_ref, sc_ref, sh_ref, w_ref, o_ref, h_ref):
    @pl.when(pl.program_id(1) == 0)
    def _():
        h = (_rms(x_ref[...]) * g_ref[...]) * (1.0 + sc_ref[...]) + sh_ref[...]
        h_ref[...] = h.astype(_BF16)

    o_ref[...] = jnp.dot(h_ref[...], w_ref[...], preferred_element_type=_F32).astype(o_ref.dtype)


def _inproj(x, g, sc, sh, w, out_dtype, tm=512, tn=1024):
    s, d = x.shape
    n = w.shape[1]
    vec = pl.BlockSpec((1, d), lambda i, j: (0, 0))
    return pl.pallas_call(
        _inproj_kernel,
        grid=(s // tm, n // tn),
        in_specs=[
            pl.BlockSpec((tm, d), lambda i, j: (i, 0)),
            vec, vec, vec,
            pl.BlockSpec((d, tn), lambda i, j: (0, j)),
        ],
        out_specs=pl.BlockSpec((tm, tn), lambda i, j: (i, j)),
        out_shape=jax.ShapeDtypeStruct((s, n), out_dtype),
        scratch_shapes=[pltpu.VMEM((tm, d), _BF16)],
        compiler_params=_params("parallel", "arbitrary"),
        name="inproj",
    )(x, g, sc, sh, w)


POS_RADIX = 256


def _attn_kernel(qi_ref, kj_ref, slope_ref,
                 q_ref, k_ref, v_ref, lq1_ref, lk1_ref, lq2_ref, lk2_ref, gs_ref,
                 o_ref,
                 qx_ref, feat_ref, mask_ref, m_ref, l_ref, acc_ref,
                 *, lambda_init, blk):
    h = pl.program_id(0)
    p = pl.program_id(1)
    i = qi_ref[p]
    j = kj_ref[p]
    slope = slope_ref[h]
    lane = jax.lax.broadcasted_iota(jnp.int32, (blk, HEAD_W), 1)
    lo_half = lane < DIFF_HEAD_DIM

    @pl.when(j == 0)
    def _init():
        qs = q_ref[...] * jnp.asarray(1.0 / math.sqrt(DIFF_HEAD_DIM), _BF16)
        sub = jnp.where(lo_half, lane, lane - DIFF_HEAD_DIM)
        coef = jnp.where(sub == 0, slope * POS_RADIX, jnp.where(sub == 1, slope, 0.0)).astype(_BF16)
        qx_ref[0] = jnp.where(lo_half, qs, coef)
        qx_ref[1] = jnp.where(lo_half, coef, qs)
        pos = jax.lax.broadcasted_iota(jnp.int32, (blk, HEAD_W), 0)
        feat = jnp.where(sub == 0, pos // POS_RADIX, jnp.where(sub == 1, pos % POS_RADIX, 0))
        feat_ref[...] = feat.astype(_F32).astype(_BF16)
        kpos = jax.lax.broadcasted_iota(jnp.int32, (blk, blk), 0)
        qpos = jax.lax.broadcasted_iota(jnp.int32, (blk, blk), 1)
        mask_ref[...] = jnp.where(kpos > qpos, _NEG_INF, 0.0).astype(_F32)
        m_ref[...] = jnp.full(m_ref.shape, _NEG_INF, _F32)
        l_ref[...] = jnp.zeros(l_ref.shape, _F32)
        acc_ref[...] = jnp.zeros(acc_ref.shape, _F32)

    def _step(diagonal):
        k = k_ref[...]
        v = v_ref[...]
        feat = feat_ref[...]
        off = slope * (j * blk).astype(_F32)
        for idx in range(2):
            kx = jnp.where(lo_half, k, feat) if idx == 0 else jnp.where(lo_half, feat, k)
            z = jax.lax.dot_general(kx, qx_ref[idx], (((1,), (1,)), ((), ())),
                                    preferred_element_type=_F32)
            if diagonal:
                z = z + mask_ref[...]
            m_prev = m_ref[idx]
            m_new = jnp.maximum(m_prev, jnp.max(z, axis=0, keepdims=True) + off)
            alpha = jnp.exp(m_prev - m_new)
            pexp = jnp.exp(z - (m_new - off))
            l_ref[idx] = alpha * l_ref[idx] + jnp.sum(pexp, axis=0, keepdims=True)
            acc_ref[idx] = alpha * acc_ref[idx] + jax.lax.dot_general(
                v, pexp.astype(_BF16), (((0,), (0,)), ((), ())), preferred_element_type=_F32)
            m_ref[idx] = m_new

    @pl.when(j < i)
    def _off_diag():
        _step(False)

    @pl.when(j == i)
    def _diag():
        _step(True)
        lam = (jnp.exp(jnp.sum(lq1_ref[...] * lk1_ref[...], axis=-1, keepdims=True))
               - jnp.exp(jnp.sum(lq2_ref[...] * lk2_ref[...], axis=-1, keepdims=True))
               + lambda_init)
        o = acc_ref[0] * (1.0 / l_ref[0]) - lam * (acc_ref[1] * (1.0 / l_ref[1]))
        o = o * jax.lax.rsqrt(jnp.mean(o * o, axis=0, keepdims=True) + EPS)
        o = (o * gs_ref[...]) * (1.0 - lambda_init)
        o_ref[...] = o.T.astype(o_ref.dtype)


def _attention(qkv, lq1, lk1, lq2, lk2, g_subln_col, lambda_init, n_heads, blk=1024):
    s = qkv.shape[0]
    nb = s // blk
    assert blk % POS_RADIX == 0 and blk // POS_RADIX <= POS_RADIX
    qi = np.concatenate([np.full(i + 1, i, np.int32) for i in range(nb)])
    kj = np.concatenate([np.arange(i + 1, dtype=np.int32) for i in range(nb)])
    slopes = (2.0 ** (-8.0 * (np.arange(n_heads, dtype=np.float32) + 1.0) / n_heads)).astype(np.float32)
    assert np.all(slopes.astype(jnp.bfloat16).astype(np.float32) == slopes)
    small = lambda n: pl.BlockSpec((1, n), lambda h, p, qi, kj, sl: (0, 0))
    grid_spec = pltpu.PrefetchScalarGridSpec(
        num_scalar_prefetch=3,
        grid=(n_heads, qi.shape[0]),
        in_specs=[
            pl.BlockSpec((blk, HEAD_W), lambda h, p, qi, kj, sl: (qi[p], h)),
            pl.BlockSpec((blk, HEAD_W), lambda h, p, qi, kj, sl: (kj[p], n_heads + h)),
            pl.BlockSpec((blk, HEAD_W), lambda h, p, qi, kj, sl: (kj[p], 2 * n_heads + h)),
            small(DIFF_HEAD_DIM), small(DIFF_HEAD_DIM), small(DIFF_HEAD_DIM), small(DIFF_HEAD_DIM),
            pl.BlockSpec((HEAD_W, 1), lambda h, p, qi, kj, sl: (0, 0)),
        ],
        out_specs=pl.BlockSpec((blk, HEAD_W), lambda h, p, qi, kj, sl: (qi[p], h)),
        scratch_shapes=[
            pltpu.VMEM((2, blk, HEAD_W), _BF16),
            pltpu.VMEM((blk, HEAD_W), _BF16),
            pltpu.VMEM((blk, blk), _F32),
            pltpu.VMEM((2, 1, blk), _F32),
            pltpu.VMEM((2, 1, blk), _F32),
            pltpu.VMEM((2, HEAD_W, blk), _F32),
        ],
    )
    return pl.pallas_call(
        functools.partial(_attn_kernel, lambda_init=lambda_init, blk=blk),
        grid_spec=grid_spec,
        out_shape=jax.ShapeDtypeStruct((s, n_heads * HEAD_W), _BF16),
        compiler_params=_params("arbitrary", "arbitrary"),
        name="attn",
    )(jnp.asarray(qi), jnp.asarray(kj), jnp.asarray(slopes),
      qkv, qkv, qkv, lq1, lk1, lq2, lk2, g_subln_col)


def _conv_kernel(cur_ref, prev_ref, w_ref, b_ref, lg_ref, lb_ref, o_ref, u_ref, *, tm, ch):
    i = pl.program_id(0)
    cur = cur_ref[...]
    u_ref[pl.ds(CONV_HALO, tm), :] = cur[:, :ch] * jax.nn.sigmoid(cur[:, ch:])
    prev = prev_ref[...]
    u_prev = prev[:, :ch] * jax.nn.sigmoid(prev[:, ch:])
    u_ref[pl.ds(0, CONV_HALO), :] = jnp.where(i > 0, u_prev, jnp.zeros_like(u_prev))
    acc = jnp.zeros((tm, ch), _F32)
    for tap in range(CONV_KERNEL):
        acc = acc + w_ref[tap:tap + 1, :] * u_ref[pl.ds(CONV_HALO - (CONV_KERNEL - 1) + tap, tm), :]
    y = acc + b_ref[...]
    mu = jnp.mean(y, axis=-1, keepdims=True)
    yc = y - mu
    var = jnp.mean(yc * yc, axis=-1, keepdims=True)
    yn = (yc * jax.lax.rsqrt(var + EPS)) * lg_ref[...] + lb_ref[...]
    o_ref[...] = (yn * jax.nn.sigmoid(yn)).astype(o_ref.dtype)


def _conv(glu, w, b, lg, lb, tm=256):
    s, two_ch = glu.shape
    ch = two_ch // 2
    ratio = tm // CONV_HALO
    vec = pl.BlockSpec((1, ch), lambda i: (0, 0))
    return pl.pallas_call(
        functools.partial(_conv_kernel, tm=tm, ch=ch),
        grid=(s // tm,),
        in_specs=[
            pl.BlockSpec((tm, two_ch), lambda i: (i, 0)),
            pl.BlockSpec((CONV_HALO, two_ch), lambda i: (jnp.maximum(i * ratio - 1, 0), 0)),
            pl.BlockSpec((CONV_KERNEL, ch), lambda i: (0, 0)),
            vec, vec, vec,
        ],
        out_specs=pl.BlockSpec((tm, ch), lambda i: (i, 0)),
        out_shape=jax.ShapeDtypeStruct((s, ch), _BF16),
        scratch_shapes=[pltpu.VMEM((tm + CONV_HALO, ch), _F32)],
        compiler_params=_params("parallel"),
        name="conv",
    )(glu, glu, w, b, lg, lb)


def _outproj_kernel(ya_ref, yc_ref, wa_ref, wc_ref, x_ref, gpost_ref, gt_ref, gpre_ref, sc_ref, sh_ref,
                    x1_ref, h2_ref, h2t_ref):
    mix = (jnp.dot(ya_ref[...], wa_ref[...], preferred_element_type=_F32)
           + jnp.dot(yc_ref[...], wc_ref[...], preferred_element_type=_F32))
    x1 = x_ref[...] + gt_ref[...] * (_rms(mix) * gpost_ref[...])
    x1_ref[...] = x1
    h2 = (_rms(x1) * gpre_ref[...]) * (1.0 + sc_ref[...]) + sh_ref[...]
    h2_ref[...] = h2.astype(h2_ref.dtype)
    h2t_ref[...] = h2.T.astype(h2t_ref.dtype)


def _outproj(ya, yc, wa, wc, x, gpost, gt, gpre, sc, sh, tm=512):
    s, d = x.shape
    ka, kc = ya.shape[1], yc.shape[1]
    vec = pl.BlockSpec((1, d), lambda i: (0, 0))
    return pl.pallas_call(
        _outproj_kernel,
        grid=(s // tm,),
        in_specs=[
            pl.BlockSpec((tm, ka), lambda i: (i, 0)),
            pl.BlockSpec((tm, kc), lambda i: (i, 0)),
            pl.BlockSpec((ka, d), lambda i: (0, 0)),
            pl.BlockSpec((kc, d), lambda i: (0, 0)),
            pl.BlockSpec((tm, d), lambda i: (i, 0)),
            vec, vec, vec, vec, vec,
        ],
        out_specs=[pl.BlockSpec((tm, d), lambda i: (i, 0)), pl.BlockSpec((tm, d), lambda i: (i, 0)),
                   pl.BlockSpec((d, tm), lambda i: (0, i))],
        out_shape=[jax.ShapeDtypeStruct((s, d), _F32), jax.ShapeDtypeStruct((s, d), _BF16),
                   jax.ShapeDtypeStruct((d, s), _BF16)],
        compiler_params=_params("parallel"),
        name="outproj",
    )(ya, yc, wa, wc, x, gpost, gt, gpre, sc, sh)


def _route_kernel(h_ref, wq_ref, k1_ref, k2_ref,
                  s2_ref, e2_ref, th_ref, e1n_ref,
                  v1_ref, v2_ref, cand_ref):
    q = jnp.dot(h_ref[...], wq_ref[...], preferred_element_type=_F32)
    nt = (((1,), (1,)), ((), ()))
    hi = jax.lax.Precision.HIGHEST
    s1 = jax.lax.dot_general(k1_ref[...], q[:, :PEER_HALF], nt, precision=hi,
                             preferred_element_type=_F32)
    s2 = jax.lax.dot_general(k2_ref[...], q[:, PEER_HALF:], nt, precision=hi,
                             preferred_element_type=_F32)

    def top_values(s, out_ref):
        work = s
        for it in range(PEER_TOPK):
            m = jnp.max(work, axis=0, keepdims=True)
            out_ref[it:it + 1, :] = m
            if it + 1 < PEER_TOPK:
                work = jnp.where(work == m, _NEG_INF, work)

    top_values(s1, v1_ref)
    top_values(s2, v2_ref)
    v1 = v1_ref[...]
    v2 = v2_ref[...]
    cand_ref[...] = jnp.full(cand_ref.shape, _NEG_INF, _F32)
    off = 0
    for a in range(PEER_TOPK):
        cnt = PEER_TOPK // (a + 1)
        cand_ref[off:off + cnt, :] = v1[a:a + 1, :] + v2[0:cnt, :]
        off += cnt
    cand = cand_ref[...]
    work = cand
    tau = None
    for it in range(PEER_TOPK):
        tau = jnp.max(work, axis=0, keepdims=True)
        if it + 1 < PEER_TOPK:
            work = jnp.where(work == tau, _NEG_INF, work)
    m0 = v1[0:1, :] + v2[0:1, :]
    z = jnp.sum(jnp.where(cand >= tau, jnp.exp(cand - m0), 0.0), axis=0, keepdims=True)
    theta = jnp.full(s1.shape, _POS_INF, _F32)
    for a in range(PEER_TOPK):
        cnt = PEER_TOPK // (a + 1)
        ca = v1[a:a + 1, :] + v2[0:cnt, :]
        th_a = jnp.min(jnp.where(ca >= tau, v2[0:cnt, :], _POS_INF), axis=0, keepdims=True)
        theta = jnp.minimum(theta, jnp.where(s1 == v1[a:a + 1, :], th_a, _POS_INF))
    s2_ref[0] = s2
    e2_ref[0] = jnp.exp(s2 - v2[0:1, :])
    th_ref[0] = theta
    e1n_ref[0] = jnp.exp(s1 - v1[0:1, :]) / z


CAND_ROWS = -(-sum(PEER_TOPK // (a + 1) for a in range(PEER_TOPK)) // 8) * 8


def _route(h2, wq, k1, k2, tm=512):
    s, d = h2.shape
    n_heads = wq.shape[1] // (2 * PEER_HALF)
    out = jax.ShapeDtypeStruct((n_heads, PEER_KEYS, s), _F32)
    ospec = pl.BlockSpec((1, PEER_KEYS, tm), lambda i, h: (h, 0, i))
    kspec = pl.BlockSpec((PEER_KEYS, PEER_HALF), lambda i, h: (0, 0))
    return pl.pallas_call(
        _route_kernel,
        grid=(s // tm, n_heads),
        in_specs=[
            pl.BlockSpec((tm, d), lambda i, h: (i, 0)),
            pl.BlockSpec((d, 2 * PEER_HALF), lambda i, h: (0, h)),
            kspec, kspec,
        ],
        out_specs=[ospec, ospec, ospec, ospec],
        out_shape=[out, out, out, out],
        scratch_shapes=[
            pltpu.VMEM((PEER_TOPK, tm), _F32),
            pltpu.VMEM((PEER_TOPK, tm), _F32),
            pltpu.VMEM((CAND_ROWS, tm), _F32),
        ],
        compiler_params=_params("parallel", "arbitrary"),
        name="route",
    )(h2, wq, k1, k2)


def _peer_kernel(ht_ref, u_ref, vt_ref, s2_ref, e2_ref, th_ref, e1n_ref, o_ref, w_ref, *, n_heads, rows):
    @pl.when(pl.program_id(1) == 0)
    def _():
        o_ref[...] = jnp.zeros(o_ref.shape, _F32)

    hu = jnp.dot(u_ref[...], ht_ref[...], preferred_element_type=_F32)
    act = 0.5 * hu * (1.0 + jax.lax.erf(hu * (1.0 / math.sqrt(2.0))))
    for r in range(rows):
        g = None
        for hd in range(n_heads):
            sel = jnp.where(s2_ref[hd] >= th_ref[hd, r:r + 1, :], e2_ref[hd], 0.0)
            term = sel * e1n_ref[hd, r:r + 1, :]
            g = term if g is None else g + term
        w_ref[r * PEER_KEYS:(r + 1) * PEER_KEYS, :] = (
            act[r * PEER_KEYS:(r + 1) * PEER_KEYS, :] * g).astype(_BF16)
    o_ref[...] += jnp.dot(vt_ref[...], w_ref[...], preferred_element_type=_F32)


def _peer(h2t, u, vt, s2t, e2t, tht, e1nt, tb=512, ec=1024):
    d, s = h2t.shape
    n_exp = u.shape[0]
    n_heads = s2t.shape[0]
    rows = ec // PEER_KEYS
    full = pl.BlockSpec((n_heads, PEER_KEYS, tb), lambda i, j: (0, 0, i))
    part = pl.BlockSpec((n_heads, rows, tb), lambda i, j: (0, j, i))
    return pl.pallas_call(
        functools.partial(_peer_kernel, n_heads=n_heads, rows=rows),
        grid=(s // tb, n_exp // ec),
        in_specs=[
            pl.BlockSpec((d, tb), lambda i, j: (0, i)),
            pl.BlockSpec((ec, d), lambda i, j: (j, 0)),
            pl.BlockSpec((d, ec), lambda i, j: (0, j)),
            full, full, part, part,
        ],
        out_specs=pl.BlockSpec((d, tb), lambda i, j: (0, i)),
        out_shape=jax.ShapeDtypeStruct((d, s), _F32),
        scratch_shapes=[pltpu.VMEM((ec, tb), _BF16)],
        compiler_params=_params("parallel", "arbitrary"),
        name="peer",
    )(h2t, u, vt, s2t, e2t, tht, e1nt)


def _final_kernel(fft_ref, x1_ref, g_ref, gt_ref, o_ref):
    ff = fft_ref[...].T
    o_ref[...] = x1_ref[...] + gt_ref[...] * (_rms(ff) * g_ref[...])


def _final(fft, x1, g, gt, tm=512):
    s, d = x1.shape
    vec = pl.BlockSpec((1, d), lambda i: (0, 0))
    return pl.pallas_call(
        _final_kernel,
        grid=(s // tm,),
        in_specs=[
            pl.BlockSpec((d, tm), lambda i: (0, i)),
            pl.BlockSpec((tm, d), lambda i: (i, 0)),
            vec, vec,
        ],
        out_specs=pl.BlockSpec((tm, d), lambda i: (i, 0)),
        out_shape=jax.ShapeDtypeStruct((s, d), _F32),
        compiler_params=_params("parallel"),
        name="final",
    )(fft, x1, g, gt)


def kernel(x, c, w_ada, b_ada, g_pre_mix, g_post_mix, g_pre_ffn, g_post_ffn, w_in, lambda_q1, lambda_k1, lambda_q2, lambda_k2, g_subln, conv_w, conv_b, conv_ln_g, conv_ln_b, w_out, peer_wq, peer_k1, peer_k2, peer_u, peer_v):
    batch, seq, d = x.shape
    depth = w_ada.shape[0]
    attn_w = d // 2
    n_heads = attn_w // HEAD_W
    assert c.shape == (batch, d) and w_in.shape[2] == 3 * attn_w + 2 * (d - attn_w)
    row = lambda a: a.reshape(1, -1)

    outs = []
    for bi in range(batch):
        xb = x[bi]
        c_col = c[bi].reshape(d, 1)
        for l in range(depth):
            lambda_init = 0.8 - 0.6 * math.exp(-0.3 * l)
            mod = _ada(c_col, w_ada[l], row(b_ada[l]))
            sh_m, sc_m, gt_m, sh_f, sc_f, gt_f = [mod[:, k * d:(k + 1) * d] for k in range(6)]

            w_in_b = w_in[l].astype(_BF16)
            g_pre = row(g_pre_mix[l])
            qkv = _inproj(xb, g_pre, sc_m, sh_m, w_in_b[:, :3 * attn_w], _BF16)
            glu = _inproj(xb, g_pre, sc_m, sh_m, w_in_b[:, 3 * attn_w:], _F32)
            y_attn = _attention(qkv, row(lambda_q1[l]), row(lambda_k1[l]), row(lambda_q2[l]),
                                row(lambda_k2[l]), g_subln[l].reshape(-1, 1), lambda_init, n_heads)
            y_conv = _conv(glu, conv_w[l], row(conv_b[l]), row(conv_ln_g[l]), row(conv_ln_b[l]))
            w_out_b = w_out[l].astype(_BF16)
            x1, h2, h2t = _outproj(y_attn, y_conv, w_out_b[:attn_w], w_out_b[attn_w:], xb,
                                   row(g_post_mix[l]), gt_m, row(g_pre_ffn[l]), sc_f, sh_f)

            s2t, e2t, tht, e1nt = _route(h2, peer_wq[l].astype(_BF16), peer_k1[l], peer_k2[l])
            fft = _peer(h2t, peer_u[l].astype(_BF16), peer_v[l].T.astype(_BF16), s2t, e2t, tht, e1nt)
            xb = _final(fft, x1, row(g_post_ffn[l]), gt_f)
        outs.append(xb)
    return jnp.stack(outs, axis=0)
```
